```python
import jax, jax.numpy as jnp
from jax import lax
import numpy as np

D_MODEL = 2048
BATCH = 4
SEQ = 4096
DEPTH = 2

GRID_W = 64
CTX_LEN = 256
N_MOD = 6
EPS = 1e-6
ROPE_THETA = 10000.0
QBLOCK = 128

MLA_HEADS = 8
MLA_NOPE = 128
MLA_ROPE = 64
MLA_QK = MLA_NOPE + MLA_ROPE
MLA_V = 128
MLA_Q_LORA = 512
MLA_KV_LORA = 256

CONV_CH = 1024
CONV_WIDTH = 31

GQA_HEADS = 8
GQA_KV_HEADS = 2
GQA_GROUP = GQA_HEADS // GQA_KV_HEADS
GQA_HD = 128

N_BRANCH = 3
BRANCH_W = 1024

OFF_KROPE = MLA_KV_LORA
OFF_GK = OFF_KROPE + MLA_ROPE
OFF_GV = OFF_GK + GQA_KV_HEADS * GQA_HD
KV_COLS = OFF_GV + GQA_KV_HEADS * GQA_HD
OFF_GQ = MLA_Q_LORA
OFF_CONV = OFF_GQ + GQA_HEADS * GQA_HD
OFF_GATE = OFF_CONV + 2 * CONV_CH
REST_COLS = OFF_GATE + N_BRANCH * D_MODEL
IN_COLS = KV_COLS + REST_COLS

PEER_HEADS = 8
PEER_KEYS = 128
PEER_EXPERTS = PEER_KEYS * PEER_KEYS
PEER_TOPK = 16
PEER_DK = 256
PEER_BLOCK = 128

kernel_name = 'hybrid_mla_conformer_gqa_peer_dit'


def rms_norm(x, g):
    xf = x.astype(jnp.float32)
    y = xf * lax.rsqrt(jnp.mean(xf * xf, axis=-1, keepdims=True) + EPS)
    return (y * g.astype(jnp.float32)).astype(x.dtype)


def layer_norm(x, g, b):
    xf = x.astype(jnp.float32)
    mu = jnp.mean(xf, axis=-1, keepdims=True)
    xc = xf - mu
    y = xc * lax.rsqrt(jnp.mean(xc * xc, axis=-1, keepdims=True) + EPS)
    return (y * g.astype(jnp.float32) + b.astype(jnp.float32)).astype(x.dtype)


def modulate(x, g, m, idx):
    return rms_norm(x, g) * (1 + m[:, :, idx + 1]) + m[:, :, idx]


def rope_tables(row, col, rot_dim):
    n_freq = rot_dim // 4
    inv = ROPE_THETA ** (-jnp.arange(n_freq, dtype=jnp.float32) / n_freq)
    ang = jnp.concatenate([row[:, None] * inv, col[:, None] * inv], axis=-1)
    return (jnp.cos(ang), jnp.sin(ang))


def apply_rope(x, cos, sin):
    half = x.shape[-1] // 2
    x1, x2 = x[..., :half], x[..., half:]
    c = cos[None, :, None, :].astype(x.dtype)
    s = sin[None, :, None, :].astype(x.dtype)
    return jnp.concatenate([x1 * c - x2 * s, x1 * s + x2 * c], axis=-1)


def rope_tail(x, cos, sin):
    return jnp.concatenate([x[..., :MLA_NOPE], apply_rope(x[..., MLA_NOPE:], cos, sin)], axis=-1)


def block_attention(q, k, v):
    b, lq, hkv, g, dq = q.shape
    nb = lq // QBLOCK
    scale = dq ** -0.5
    qb = q.reshape(b, nb, QBLOCK, hkv, g, dq).transpose(1, 0, 2, 3, 4, 5)

    def one(qi):
        s = jnp.einsum('bqhgd,bshd->bhgqs', qi, k).astype(jnp.float32) * scale
        p = jax.nn.softmax(s, axis=-1).astype(v.dtype)
        return jnp.einsum('bhgqs,bshd->bqhgd', p, v)

    o = lax.map(one, qb)
    return o.transpose(1, 0, 2, 3, 4, 5).reshape(b, lq, hkv * g * v.shape[-1])


def kv_heads(h, lw):
    b, l, _ = h.shape
    z = h @ lw['w_in'][:, :KV_COLS]
    ckv = z[..., :OFF_KROPE]
    krope = z[..., OFF_KROPE:OFF_GK]
    gk = z[..., OFF_GK:OFF_GV]
    gv = z[..., OFF_GV:]
    kv = (rms_norm(ckv, lw['mla_ckv_g']) @ lw['mla_w_ukv']).reshape(b, l, MLA_HEADS, MLA_NOPE + MLA_V)
    k_nope, v_m = kv[..., :MLA_NOPE], kv[..., MLA_NOPE:]
    k_r = jnp.broadcast_to(krope[:, :, None, :], (b, l, MLA_HEADS, MLA_ROPE))
    k_m = rms_norm(jnp.concatenate([k_nope, k_r], axis=-1), lw['mla_k_g'])
    k_g = rms_norm(gk.reshape(b, l, GQA_KV_HEADS, GQA_HD), lw['gqa_k_g'])
    v_g = gv.reshape(b, l, GQA_KV_HEADS, GQA_HD)
    return (k_m, v_m, k_g, v_g)


def conformer_conv(u, lw):
    glu = u[..., :CONV_CH] * jax.nn.sigmoid(u[..., CONV_CH:])
    pad = CONV_WIDTH // 2
    y = lax.conv_general_dilated(glu, lw['dw_w'][:, None, :], window_strides=(1,), padding=[(pad, pad)],
                                 dimension_numbers=('NWC', 'WIO', 'NWC'), feature_group_count=CONV_CH)
    y = layer_norm(y + lw['dw_b'], lw['ln_g'], lw['ln_b'])
    return jax.nn.silu(y)


def mix(h, kv, ctx_kv, rope, lw):
    b, l, _ = h.shape
    z = h @ lw['w_in'][:, KV_COLS:]
    cq = z[..., :OFF_GQ]
    gq = z[..., OFF_GQ:OFF_CONV]
    conv_in = z[..., OFF_CONV:OFF_GATE]
    gates = jax.nn.sigmoid(z[..., OFF_GATE:]).reshape(b, l, N_BRANCH, D_MODEL)
    q_m = (rms_norm(cq, lw['mla_cq_g']) @ lw['mla_w_uq']).reshape(b, l, MLA_HEADS, MLA_QK)
    q_m = rms_norm(q_m, lw['mla_q_g'])
    q_g = rms_norm(gq.reshape(b, l, GQA_HEADS, GQA_HD), lw['gqa_q_g'])
    k_m, v_m, k_g, v_g = kv
    if rope is not None:
        cos_m, sin_m, cos_g, sin_g = rope
        q_m = rope_tail(q_m, cos_m, sin_m)
        k_m = rope_tail(k_m, cos_m, sin_m)
        q_g = apply_rope(q_g, cos_g, sin_g)
        k_g = apply_rope(k_g, cos_g, sin_g)
        ck_m, cv_m, ck_g, cv_g = ctx_kv
        k_m = jnp.concatenate([ck_m, k_m], axis=1)
        v_m = jnp.concatenate([cv_m, v_m], axis=1)
        k_g = jnp.concatenate([ck_g, k_g], axis=1)
        v_g = jnp.concatenate([cv_g, v_g], axis=1)
    o_m = block_attention(q_m[:, :, :, None, :], k_m, v_m)
    o_g = block_attention(q_g.reshape(b, l, GQA_KV_HEADS, GQA_GROUP, GQA_HD), k_g, v_g)
    o_c = conformer_conv(conv_in, lw)
    wb = lw['w_branch']
    y = (gates[:, :, 0] * (o_m @ wb[0]) + gates[:, :, 1] * (o_c @ wb[1]) + gates[:, :, 2] * (o_g @ wb[2]))
    return y @ lw['w_out']


def peer(h, w_q, subkeys, u_tab, v_tab):
    b, l, d = h.shape
    tb = h.reshape(-1, PEER_BLOCK, d)

    def one(t):
        q = (t @ w_q).reshape(PEER_BLOCK, PEER_HEADS, 2, PEER_DK // 2)
        s = jnp.einsum('thpd,hpkd->thpk', q, subkeys).astype(jnp.float32)
        s_top, i_top = lax.top_k(s, PEER_TOPK)
        cand = (s_top[:, :, 0, :, None] + s_top[:, :, 1, None, :]).reshape(PEER_BLOCK, PEER_HEADS, PEER_TOPK * PEER_TOPK)
        cand_id = (i_top[:, :, 0, :, None] * PEER_KEYS + i_top[:, :, 1, None, :]).reshape(PEER_BLOCK, PEER_HEADS, PEER_TOPK * PEER_TOPK)
        best, pos = lax.top_k(cand, PEER_TOPK)
        ids = jnp.take_along_axis(cand_id, pos, axis=-1)
        g = jax.nn.softmax(best, axis=-1).astype(t.dtype)
        a = jnp.einsum('thkd,td->thk', jnp.take(u_tab, ids, axis=0), t)
        coef = g * jax.nn.gelu(a)
        return jnp.einsum('thk,thkd->td', coef, jnp.take(v_tab, ids, axis=0))

    return lax.map(one, tb).reshape(b, l, d)


def setup_inputs(seed: int = 0) -> dict:
    key = jax.random.key(seed)
    ks = jax.random.split(key, 27)
    f32 = jnp.float32
    D = D_MODEL

    def nrm(k, shape, scale):
        return jax.random.normal(k, shape, f32) * scale

    def gain(k, shape):
        return 1.0 + 0.02 * jax.random.normal(k, shape, f32)

    return {
        'x': nrm(ks[0], (BATCH, SEQ, D), 1.0),
        'c': nrm(ks[1], (BATCH, D), 1.0),
        'ctx': nrm(ks[2], (BATCH, CTX_LEN, D), 1.0),
        'c_ctx': nrm(ks[3], (D,), 1.0),
        'ada_w': nrm(ks[4], (DEPTH, D, N_MOD * D), 0.5 * D ** -0.5),
        'ada_b': nrm(ks[5], (DEPTH, N_MOD * D), 0.02),
        'norm1_g': gain(ks[6], (DEPTH, D)),
        'norm2_g': gain(ks[7], (DEPTH, D)),
        'w_in': nrm(ks[8], (DEPTH, D, IN_COLS), D ** -0.5),
        'mla_cq_g': gain(ks[9], (DEPTH, MLA_Q_LORA)),
        'mla_ckv_g': gain(ks[10], (DEPTH, MLA_KV_LORA)),
        'mla_w_uq': nrm(ks[11], (DEPTH, MLA_Q_LORA, MLA_HEADS * MLA_QK), MLA_Q_LORA ** -0.5),
        'mla_w_ukv': nrm(ks[12], (DEPTH, MLA_KV_LORA, MLA_HEADS * (MLA_NOPE + MLA_V)), MLA_KV_LORA ** -0.5),
        'mla_q_g': gain(ks[13], (DEPTH, MLA_QK)),
        'mla_k_g': gain(ks[14], (DEPTH, MLA_QK)),
        'gqa_q_g': gain(ks[15], (DEPTH, GQA_HD)),
        'gqa_k_g': gain(ks[16], (DEPTH, GQA_HD)),
        'conv_dw_w': nrm(ks[17], (DEPTH, CONV_WIDTH, CONV_CH), CONV_WIDTH ** -0.5),
        'conv_dw_b': nrm(ks[18], (DEPTH, CONV_CH), 0.02),
        'conv_ln_g': gain(ks[19], (DEPTH, CONV_CH)),
        'conv_ln_b': nrm(ks[20], (DEPTH, CONV_CH), 0.02),
        'w_branch': nrm(ks[21], (DEPTH, N_BRANCH, BRANCH_W, D), BRANCH_W ** -0.5),
        'w_out': nrm(ks[22], (DEPTH, D, D), D ** -0.5),
        'peer_w_q': nrm(ks[23], (DEPTH, D, PEER_HEADS * PEER_DK), D ** -0.5),
        'peer_subkeys': nrm(ks[24], (DEPTH, PEER_HEADS, 2, PEER_KEYS, PEER_DK // 2), (PEER_DK // 2) ** -0.5),
        'peer_u': nrm(ks[25], (DEPTH, PEER_EXPERTS, D), D ** -0.5),
        'peer_v': nrm(ks[26], (DEPTH, PEER_EXPERTS, D), 1.0),
    }


def reference(x, c, ctx, c_ctx, ada_w, ada_b, norm1_g, norm2_g, w_in, mla_cq_g, mla_ckv_g, mla_w_uq, mla_w_ukv,
              mla_q_g, mla_k_g, gqa_q_g, gqa_k_g, conv_dw_w, conv_dw_b, conv_ln_g, conv_ln_b, w_branch, w_out,
              peer_w_q, peer_subkeys, peer_u, peer_v):
    n_lat = x.shape[1]
    ROWS = n_lat // GRID_W
    row = jnp.repeat(jnp.arange(ROWS, dtype=jnp.float32), GRID_W)
    col = jnp.tile(jnp.arange(GRID_W, dtype=jnp.float32), ROWS)
    rope = rope_tables(row, col, MLA_ROPE) + rope_tables(row, col, GQA_HD)
    x_lat, x_ctx = x, ctx
    for i in range(DEPTH):
        lw = {'w_in': w_in[i], 'mla_cq_g': mla_cq_g[i], 'mla_ckv_g': mla_ckv_g[i], 'mla_w_uq': mla_w_uq[i],
              'mla_w_ukv': mla_w_ukv[i], 'mla_q_g': mla_q_g[i], 'mla_k_g': mla_k_g[i], 'gqa_q_g': gqa_q_g[i],
              'gqa_k_g': gqa_k_g[i], 'dw_w': conv_dw_w[i], 'dw_b': conv_dw_b[i], 'ln_g': conv_ln_g[i],
              'ln_b': conv_ln_b[i], 'w_branch': w_branch[i], 'w_out': w_out[i]}
        m_lat = (jax.nn.silu(c) @ ada_w[i] + ada_b[i]).reshape(c.shape[0], 1, N_MOD, D_MODEL)
        m_ctx = (jax.nn.silu(c_ctx) @ ada_w[i] + ada_b[i]).reshape(1, 1, N_MOD, D_MODEL)
        h_ctx = modulate(x_ctx, norm1_g[i], m_ctx, 0)
        ctx_kv = kv_heads(h_ctx, lw)
        h_lat = modulate(x_lat, norm1_g[i], m_lat, 0)
        x_lat = x_lat + m_lat[:, :, 2] * mix(h_lat, kv_heads(h_lat, lw), ctx_kv, rope, lw)
        x_lat = x_lat + m_lat[:, :, 5] * peer(modulate(x_lat, norm2_g[i], m_lat, 3),
                                              peer_w_q[i], peer_subkeys[i], peer_u[i], peer_v[i])
        if i < DEPTH - 1:
            x_ctx = x_ctx + m_ctx[:, :, 2] * mix(h_ctx, ctx_kv, None, None, lw)
            x_ctx = x_ctx + m_ctx[:, :, 5] * peer(modulate(x_ctx, norm2_g[i], m_ctx, 3),
                                                  peer_w_q[i], peer_subkeys[i], peer_u[i], peer_v[i])
    return x_lat
```

```python
import functools

import jax
import jax.numpy as jnp
from jax import lax
from jax.experimental import pallas as pl
from jax.experimental.pallas import tpu as pltpu

F32 = jnp.float32
BF16 = jnp.bfloat16

D_MODEL = 2048
N_MOD = 6
EPS = 1e-6
GRID_W = 64
ROPE_THETA = 10000.0

MLA_HEADS = 8
MLA_NOPE = 128
MLA_ROPE = 64
MLA_QK = MLA_NOPE + MLA_ROPE
MLA_V = 128
MLA_Q_LORA = 512
MLA_KV_LORA = 256
MLA_HEAD_PAD = 256

CONV_CH = 1024
CONV_WIDTH = 31
CONV_HALO = 16

GQA_HEADS = 8
GQA_KV_HEADS = 2
GQA_GROUP = GQA_HEADS // GQA_KV_HEADS
GQA_HD = 128

N_BRANCH = 3
BRANCH_W = 1024

OFF_KROPE = MLA_KV_LORA
OFF_GK = OFF_KROPE + MLA_ROPE
OFF_GV = OFF_GK + GQA_KV_HEADS * GQA_HD
KV_COLS = OFF_GV + GQA_KV_HEADS * GQA_HD
OFF_GQ = MLA_Q_LORA
OFF_CONV = OFF_GQ + GQA_HEADS * GQA_HD
OFF_GATE = OFF_CONV + 2 * CONV_CH

PEER_HEADS = 8
PEER_KEYS = 128
PEER_EXPERTS = PEER_KEYS * PEER_KEYS
PEER_TOPK = 16
PEER_DK = 256

LANES = 128
MOD_ROWS = 16
CTX_ROW = 4
VMEM_LIMIT = 56 * 1024 * 1024
NEG_INF = float("-inf")

_NT = (((1,), (1,)), ((), ()))


def _params(*sem):
    return pltpu.CompilerParams(dimension_semantics=sem, vmem_limit_bytes=VMEM_LIMIT)


def _tile(n, want):
    t = min(n, want)
    assert n % t == 0, (n, t)
    return t


def _dot(a, b):
    return jnp.dot(a, b, preferred_element_type=F32)


def _rms(x, n):
    return x * lax.rsqrt(jnp.sum(x * x, axis=-1, keepdims=True) * (1.0 / n) + EPS)


def _rope(x, cos, sin_signed):
    return x * cos + pltpu.roll(x, LANES // 2, 1) * sin_signed


def _modvec_kernel(c_ref, w_ref, b_ref, o_ref):
    c = c_ref[...]
    a = c * jax.nn.sigmoid(c)
    a_hi = a.astype(BF16)
    a_lo = (a - a_hi.astype(F32)).astype(BF16)
    w = w_ref[...]
    w_hi = w.astype(BF16)
    w_lo = (w - w_hi.astype(F32)).astype(BF16)
    o_ref[...] = _dot(a_hi, w_hi) + _dot(a_lo, w_hi) + _dot(a_hi, w_lo) + b_ref[...]


def _modvec(cc, ada_w, ada_b):
    depth, d, n = ada_w.shape
    tn = _tile(n, 1024)
    return pl.pallas_call(
        _modvec_kernel,
        grid=(depth, n // tn),
        in_specs=[
            pl.BlockSpec((MOD_ROWS, d), lambda l, j: (0, 0)),
            pl.BlockSpec((None, d, tn), lambda l, j: (l, 0, j)),
            pl.BlockSpec((None, 1, tn), lambda l, j: (l, 0, j)),
        ],
        out_specs=pl.BlockSpec((None, MOD_ROWS, tn), lambda l, j: (l, 0, j)),
        out_shape=jax.ShapeDtypeStruct((depth, MOD_ROWS, n), F32),
        compiler_params=_params("parallel", "parallel"),
        name="modvec",
    )(cc, ada_w, ada_b.reshape(depth, 1, n))


def _norm_mod_kernel(x_ref, g_ref, sh_ref, sc_ref, o_ref):
    y = _rms(x_ref[...], D_MODEL) * g_ref[...]
    o_ref[...] = (y * (1.0 + sc_ref[...]) + sh_ref[...]).astype(BF16)


def _norm_mod(x, g, shift, scale, row_of):
    t, d = x.shape
    tm = _tile(t, 512)
    mod_spec = pl.BlockSpec((None, 1, d), lambda i: (row_of(i, tm), 0, 0))
    return pl.pallas_call(
        _norm_mod_kernel,
        grid=(t // tm,),
        in_specs=[
            pl.BlockSpec((tm, d), lambda i: (i, 0)),
            pl.BlockSpec((1, d), lambda i: (0, 0)),
            mod_spec,
            mod_spec,
        ],
        out_specs=pl.BlockSpec((tm, d), lambda i: (i, 0)),
        out_shape=jax.ShapeDtypeStruct((t, d), BF16),
        compiler_params=_params("parallel"),
        name="norm_mod",
    )(x, g, shift, scale)


def _kv_kernel(h_ref, wkv_ref, ckvg_ref, wukv_ref, kgn_ref, kgr_ref, gkg_ref,
               cm_ref, sm_ref, cg_ref, sg_ref, km_ref, vm_ref, kg_ref, vg_ref):
    z = _dot(h_ref[...], wkv_ref[...])
    ckv = _rms(z[:, :MLA_KV_LORA], MLA_KV_LORA) * ckvg_ref[...]
    kv = _dot(ckv.astype(BF16), wukv_ref[...])
    kr = z[:, 768:896]
    kr_ss = jnp.sum(kr * kr, axis=-1, keepdims=True)
    kr_rot = _rope(kr * kgr_ref[...], cm_ref[...], sm_ref[...])
    kgn = kgn_ref[...]
    for h in range(MLA_HEADS):
        c0 = h * MLA_HEAD_PAD
        kn = kv[:, c0:c0 + MLA_NOPE]
        r = lax.rsqrt((jnp.sum(kn * kn, axis=-1, keepdims=True) + kr_ss) * (1.0 / MLA_QK) + EPS)
        km_ref[:, c0:c0 + MLA_NOPE] = (kn * r * kgn).astype(BF16)
        km_ref[:, c0 + MLA_NOPE:c0 + MLA_HEAD_PAD] = (kr_rot * r).astype(BF16)
        vm_ref[:, h * MLA_V:(h + 1) * MLA_V] = kv[:, c0 + MLA_NOPE:c0 + MLA_HEAD_PAD].astype(BF16)
    for j in range(GQA_KV_HEADS):
        gk = _rms(z[:, 256 + j * GQA_HD:256 + (j + 1) * GQA_HD], GQA_HD) * gkg_ref[...]
        kg_ref[:, j * GQA_HD:(j + 1) * GQA_HD] = _rope(gk, cg_ref[...], sg_ref[...]).astype(BF16)
    vg_ref[...] = z[:, 512:768].astype(BF16)


def _kv_heads(h, lw, tabs, n_pos):
    t, d = h.shape
    tm = _tile(n_pos, 512)
    npt = n_pos // tm
    full = lambda a: pl.BlockSpec(a.shape, lambda i: (0,) * a.ndim)
    tab = pl.BlockSpec((tm, LANES), lambda i: (i % npt, 0))
    row = lambda w: pl.BlockSpec((tm, w), lambda i: (i, 0))
    ws = [lw["w_kv"], lw["ckv_g"], lw["w_ukv"], lw["k_g_nope"], lw["k_g_rope"], lw["gqa_k_g"]]
    return pl.pallas_call(
        _kv_kernel,
        grid=(t // tm,),
        in_specs=[row(d)] + [full(w) for w in ws] + [tab] * 4,
        out_specs=[row(MLA_HEADS * MLA_HEAD_PAD), row(MLA_HEADS * MLA_V),
                   row(GQA_KV_HEADS * GQA_HD), row(GQA_KV_HEADS * GQA_HD)],
        out_shape=[jax.ShapeDtypeStruct((t, MLA_HEADS * MLA_HEAD_PAD), BF16),
                   jax.ShapeDtypeStruct((t, MLA_HEADS * MLA_V), BF16),
                   jax.ShapeDtypeStruct((t, GQA_KV_HEADS * GQA_HD), BF16),
                   jax.ShapeDtypeStruct((t, GQA_KV_HEADS * GQA_HD), BF16)],
        compiler_params=_params("parallel"),
        name="kv_heads",
    )(h, *ws, *tabs)


def _q_kernel(h_ref, wq_ref, cqg_ref, wuq_ref, qgm_ref, qgg_ref,
              cm_ref, sm_ref, cg_ref, sg_ref, qm_ref, qg_ref):
    z = _dot(h_ref[...], wq_ref[...])
    cq = _rms(z[:, :MLA_Q_LORA], MLA_Q_LORA) * cqg_ref[...]
    qq = _dot(cq.astype(BF16), wuq_ref[...])
    qgm = qgm_ref[...]
    for h in range(MLA_HEADS):
        c0 = h * MLA_HEAD_PAD
        seg = _rms(qq[:, c0:c0 + MLA_HEAD_PAD], MLA_QK) * qgm * (MLA_QK ** -0.5)
        qm_ref[:, c0:c0 + MLA_NOPE] = seg[:, :MLA_NOPE].astype(BF16)
        qm_ref[:, c0 + MLA_NOPE:c0 + MLA_HEAD_PAD] = _rope(
            seg[:, MLA_NOPE:], cm_ref[...], sm_ref[...]).astype(BF16)
    for h in range(GQA_HEADS):
        c0 = MLA_Q_LORA + h * GQA_HD
        seg = _rms(z[:, c0:c0 + GQA_HD], GQA_HD) * qgg_ref[...] * (GQA_HD ** -0.5)
        qg_ref[:, h * GQA_HD:(h + 1) * GQA_HD] = _rope(seg, cg_ref[...], sg_ref[...]).astype(BF16)


def _q_heads(h, lw, tabs, n_pos):
    t, d = h.shape
    tm = _tile(n_pos, 512)
    npt = n_pos // tm
    full = lambda a: pl.BlockSpec(a.shape, lambda i: (0,) * a.ndim)
    tab = pl.BlockSpec((tm, LANES), lambda i: (i % npt, 0))
    row = lambda w: pl.BlockSpec((tm, w), lambda i: (i, 0))
    ws = [lw["w_q"], lw["cq_g"], lw["w_uq"], lw["q_g_mla"], lw["gqa_q_g"]]
    return pl.pallas_call(
        _q_kernel,
        grid=(t // tm,),
        in_specs=[row(d)] + [full(w) for w in ws] + [tab] * 4,
        out_specs=[row(MLA_HEADS * MLA_HEAD_PAD), row(GQA_HEADS * GQA_HD)],
        out_shape=[jax.ShapeDtypeStruct((t, MLA_HEADS * MLA_HEAD_PAD), BF16),
                   jax.ShapeDtypeStruct((t, GQA_HEADS * GQA_HD), BF16)],
        compiler_params=_params("parallel"),
        name="q_heads",
    )(h, *ws, *tabs)


def _attn_kernel(*refs, n_kv, group, dq, dv):
    q_ref, o_ref = refs[0], refs[-1]
    k_refs = refs[1:1 + n_kv]
    v_refs = refs[1 + n_kv:1 + 2 * n_kv]
    for g in range(group):
        q = q_ref[:, g * dq:(g + 1) * dq]
        s = [lax.dot_general(q, k[...], _NT, preferred_element_type=F32) for k in k_refs]
        m = functools.reduce(jnp.maximum, [jnp.max(x, axis=-1, keepdims=True) for x in s])
        p = [jnp.exp(x - m) for x in s]
        l = functools.reduce(jnp.add, [jnp.sum(x, axis=-1, keepdims=True) for x in p])
        o = functools.reduce(jnp.add, [_dot(x.astype(BF16), v[...]) for x, v in zip(p, v_refs)])
        o_ref[:, g * dv:(g + 1) * dv] = (o / l).astype(BF16)


def _attention(q, ks, vs, n_kv_heads, group, dq, dv):
    b, l, _ = q.shape
    tq = _tile(l, 512)
    kspec = lambda a, w: pl.BlockSpec((None, a.shape[1], w), lambda bi, hi, i: (bi, 0, hi))
    return pl.pallas_call(
        functools.partial(_attn_kernel, n_kv=len(ks), group=group, dq=dq, dv=dv),
        grid=(b, n_kv_heads, l // tq),
        in_specs=[pl.BlockSpec((None, tq, group * dq), lambda bi, hi, i: (bi, i, hi))]
        + [kspec(k, dq) for k in ks] + [kspec(v, dv) for v in vs],
        out_specs=pl.BlockSpec((None, tq, group * dv), lambda bi, hi, i: (bi, i, hi)),
        out_shape=jax.ShapeDtypeStruct((b, l, n_kv_heads * group * dv), BF16),
        compiler_params=_params("parallel", "parallel", "parallel"),
        name="attention",
    )(q, *ks, *vs)


def _glu_kernel(h_ref, wa_ref, wb_ref, o_ref):
    h = h_ref[...]
    o_ref[...] = (_dot(h, wa_ref[...]) * jax.nn.sigmoid(_dot(h, wb_ref[...]))).astype(BF16)


def _glu(h, wa, wb):
    t, d = h.shape
    tm = _tile(t, 512)
    full = lambda a: pl.BlockSpec(a.shape, lambda i: (0, 0))
    return pl.pallas_call(
        _glu_kernel,
        grid=(t // tm,),
        in_specs=[pl.BlockSpec((tm, d), lambda i: (i, 0)), full(wa), full(wb)],
        out_specs=pl.BlockSpec((tm, CONV_CH), lambda i: (i, 0)),
        out_shape=jax.ShapeDtypeStruct((t, CONV_CH), BF16),
        compiler_params=_params("parallel"),
        name="glu",
    )(h, wa, wb)


def _conv_kernel(x_ref, w_ref, b_ref, g_ref, beta_ref, o_ref, pad_ref, *, tl):
    i = pl.program_id(1)
    last = pl.num_programs(1) - 1
    start = pl.multiple_of(i * tl, tl)
    pad_ref[CONV_HALO:CONV_HALO + tl, :] = x_ref[pl.ds(start, tl), :].astype(F32)
    zeros = jnp.zeros((CONV_HALO, CONV_CH), F32)

    @pl.when(i == 0)
    def _():
        pad_ref[0:CONV_HALO, :] = zeros

    @pl.when(i > 0)
    def _():
        pad_ref[0:CONV_HALO, :] = x_ref[pl.ds(start - CONV_HALO, CONV_HALO), :].astype(F32)

    @pl.when(i == last)
    def _():
        pad_ref[CONV_HALO + tl:, :] = zeros

    @pl.when(i < last)
    def _():
        pad_ref[CONV_HALO + tl:, :] = x_ref[pl.ds(start + tl, CONV_HALO), :].astype(F32)

    off = CONV_HALO - CONV_WIDTH // 2
    acc = jnp.zeros((tl, CONV_CH), F32) + b_ref[...]
    for k in range(CONV_WIDTH):
        acc = acc + pad_ref[off + k:off + k + tl, :] * w_ref[k:k + 1, :]
    mu = jnp.mean(acc, axis=-1, keepdims=True)
    xc = acc - mu
    y = xc * lax.rsqrt(jnp.mean(xc * xc, axis=-1, keepdims=True) + EPS)
    y = y * g_ref[...] + beta_ref[...]
    o_ref[...] = (y * jax.nn.sigmoid(y)).astype(BF16)


def _conv(glu, w, b, g, beta):
    bsz, l, ch = glu.shape
    tl = _tile(l, 256)
    vec = pl.BlockSpec((1, ch), lambda bi, i: (0, 0))
    return pl.pallas_call(
        functools.partial(_conv_kernel, tl=tl),
        grid=(bsz, l // tl),
        in_specs=[pl.BlockSpec((None, l, ch), lambda bi, i: (bi, 0, 0)),
                  pl.BlockSpec((CONV_WIDTH, ch), lambda bi, i: (0, 0)), vec, vec, vec],
        out_specs=pl.BlockSpec((None, tl, ch), lambda bi, i: (bi, i, 0)),
        out_shape=jax.ShapeDtypeStruct((bsz, l, ch), BF16),
        scratch_shapes=[pltpu.VMEM((tl + 2 * CONV_HALO, ch), F32)],
        compiler_params=_params("parallel", "arbitrary"),
        name="conv",
    )(glu, w, b, g, beta)


def _merge_kernel(h_ref, om_ref, oc_ref, og_ref, g0_ref, g1_ref, g2_ref, wb_ref, y_ref):
    h = h_ref[...]
    y = jax.nn.sigmoid(_dot(h, g0_ref[...])) * _dot(om_ref[...], wb_ref[0])
    y = y + jax.nn.sigmoid(_dot(h, g1_ref[...])) * _dot(oc_ref[...], wb_ref[1])
    y = y + jax.nn.sigmoid(_dot(h, g2_ref[...])) * _dot(og_ref[...], wb_ref[2])
    y_ref[...] = y.astype(BF16)


def _merge(h, o_m, o_c, o_g, w_gate, w_branch):
    t, d = h.shape
    tm = _tile(t, 512)
    tn = 512
    nb = d // tn
    row = lambda w: pl.BlockSpec((tm, w), lambda i, j: (i, 0))
    gate = lambda br: pl.BlockSpec((d, tn), lambda i, j: (0, br * nb + j))
    return pl.pallas_call(
        _merge_kernel,
        grid=(t // tm, nb),
        in_specs=[row(d), row(BRANCH_W), row(BRANCH_W), row(BRANCH_W), gate(0), gate(1), gate(2),
                  pl.BlockSpec((N_BRANCH, BRANCH_W, tn), lambda i, j: (0, 0, j))],
        out_specs=pl.BlockSpec((tm, tn), lambda i, j: (i, j)),
        out_shape=jax.ShapeDtypeStruct((t, d), BF16),
        compiler_params=_params("parallel", "arbitrary"),
        name="merge",
    )(h, o_m, o_c, o_g, w_gate, w_gate, w_gate, w_branch)


def _out_kernel(y_ref, w_ref, x_ref, gate_ref, g_ref, sh_ref, sc_ref, xo_ref, ho_ref):
    x = x_ref[...] + gate_ref[...] * _dot(y_ref[...], w_ref[...])
    xo_ref[...] = x
    hn = _rms(x, D_MODEL) * g_ref[...]
    ho_ref[...] = (hn * (1.0 + sc_ref[...]) + sh_ref[...]).astype(BF16)


def _out_proj(y, w_out, x, gate, g2, shift, scale, row_of):
    t, d = x.shape
    tm = _tile(t, 256)
    row = pl.BlockSpec((tm, d), lambda i: (i, 0))
    mod = pl.BlockSpec((None, 1, d), lambda i: (row_of(i, tm), 0, 0))
    return pl.pallas_call(
        _out_kernel,
        grid=(t // tm,),
        in_specs=[row, pl.BlockSpec((d, d), lambda i: (0, 0)), row, mod,
                  pl.BlockSpec((1, d), lambda i: (0, 0)), mod, mod],
        out_specs=[row, row],
        out_shape=[jax.ShapeDtypeStruct((t, d), F32), jax.ShapeDtypeStruct((t, d), BF16)],
        compiler_params=_params("parallel"),
        name="out_proj",
    )(y, w_out, x, gate, g2, shift, scale)


def _top16_sorted(x, iota, top_ref):
    def body(it, x):
        m = jnp.max(x, axis=0, keepdims=True)
        first = jnp.min(jnp.where(x == m, iota, PEER_KEYS), axis=0, keepdims=True)
        top_ref[pl.ds(it, 1), :] = m
        return jnp.where(iota == first, NEG_INF, x)

    lax.fori_loop(0, PEER_TOPK, body, x)


def _kth_largest_and_z(cand, iota, cmax):
    n_rows = cand.shape[0]

    def body(_, carry):
        c, _, z = carry
        m = jnp.max(c, axis=0, keepdims=True)
        first = jnp.min(jnp.where(c == m, iota, n_rows), axis=0, keepdims=True)
        return jnp.where(iota == first, NEG_INF, c), m, z + jnp.exp(m - cmax)

    zero = jnp.zeros_like(cmax)
    _, kth, z = lax.fori_loop(0, PEER_TOPK, body, (cand, zero, zero))
    return kth, z


_CAND_ROWS = PEER_TOPK + 7 * 8 + 8


def _peer_select_kernel(h_ref, wq_ref, sub_ref, s0_ref, e0_ref, s1_ref, e1_ref, thr_ref,
                        top0_ref, top1_ref, cand_ref):
    tm = h_ref.shape[0]
    q = _dot(h_ref[...], wq_ref[...]).astype(BF16)
    iota = lax.broadcasted_iota(jnp.int32, (PEER_KEYS, tm), 0)
    cand_iota = lax.broadcasted_iota(jnp.int32, (_CAND_ROWS, tm), 0)
    for h in range(PEER_HEADS):
        s = []
        for p, top_ref in enumerate((top0_ref, top1_ref)):
            hp = 2 * h + p
            sc = lax.dot_general(sub_ref[hp], q[:, hp * LANES:(hp + 1) * LANES], _NT,
                                 preferred_element_type=F32)
            s.append(sc)
            _top16_sorted(sc, iota, top_ref)
        cand_ref[0:PEER_TOPK, :] = top0_ref[0:1, :] + top1_ref[...]
        for a in range(1, 8):
            cand_ref[8 + 8 * a:16 + 8 * a, :] = top0_ref[a:a + 1, :] + top1_ref[0:8, :]
        cand_ref[72:80, :] = top0_ref[8:16, :] + top1_ref[0:1, :]
        m0, m1 = top0_ref[0:1, :], top1_ref[0:1, :]
        thr, z = _kth_largest_and_z(cand_ref[...], cand_iota, m0 + m1)
        s0_ref[h] = s[0]
        s1_ref[h] = s[1]
        e0_ref[h] = jnp.exp(s[0] - m0)
        e1_ref[h] = jnp.exp(s[1] - m1) / z
        thr_ref[h] = jnp.broadcast_to(thr, (8, tm))


def _peer_select(h2, w_pq, subkeys):
    t, d = h2.shape
    tm = _tile(t, 256)
    big = pl.BlockSpec((PEER_HEADS, PEER_KEYS, tm), lambda i: (0, 0, i))
    big_shape = jax.ShapeDtypeStruct((PEER_HEADS, PEER_KEYS, t), F32)
    return pl.pallas_call(
        _peer_select_kernel,
        grid=(t // tm,),
        in_specs=[pl.BlockSpec((tm, d), lambda i: (i, 0)),
                  pl.BlockSpec(w_pq.shape, lambda i: (0, 0)),
                  pl.BlockSpec(subkeys.shape, lambda i: (0, 0, 0))],
        out_specs=[big, big, big, big, pl.BlockSpec((PEER_HEADS, 8, tm), lambda i: (0, 0, i))],
        out_shape=[big_shape, big_shape, big_shape, big_shape,
                   jax.ShapeDtypeStruct((PEER_HEADS, 8, t), F32)],
        scratch_shapes=[pltpu.VMEM((PEER_TOPK, tm), F32), pltpu.VMEM((PEER_TOPK, tm), F32),
                        pltpu.VMEM((_CAND_ROWS, tm), F32)],
        compiler_params=_params("parallel"),
        name="peer_select",
    )(h2, w_pq, subkeys)


def _gelu_tanh(x):
    return 0.5 * x * (1.0 + jnp.tanh(0.7978845608028654 * (x + 0.044715 * (x * x * x))))


def _peer_expert_kernel(h_ref, u_ref, vt_ref, s0_ref, e0_ref, s1_ref, e1_ref, thr_ref,
                        x_ref, gate_ref, o_ref, acc_ref, coef_ref, *, n_sub):
    e = pl.program_id(1)

    @pl.when(e == 0)
    def _():
        acc_ref[...] = jnp.zeros_like(acc_ref)

    a_t = lax.dot_general(u_ref[...], h_ref[...], _NT, preferred_element_type=F32)
    for ii in range(n_sub):
        i = e * n_sub + ii
        w = jnp.zeros((PEER_KEYS, h_ref.shape[0]), F32)
        for h in range(PEER_HEADS):
            total = s1_ref[h] + s0_ref[h, pl.ds(i, 1), :]
            val = e1_ref[h] * e0_ref[h, pl.ds(i, 1), :]
            w = w + jnp.where(total >= thr_ref[h, 0:1, :], val, 0.0)
        a = a_t[ii * PEER_KEYS:(ii + 1) * PEER_KEYS]
        coef_ref[ii * PEER_KEYS:(ii + 1) * PEER_KEYS, :] = (w * _gelu_tanh(a)).astype(BF16)
    acc_ref[...] += _dot(vt_ref[...], coef_ref[...])

    @pl.when(e == pl.num_programs(1) - 1)
    def _():
        o_ref[...] = x_ref[...] + gate_ref[...] * acc_ref[...].T


def _peer_experts(h2, u, vt, sel, x, gate, row_of):
    t, d = h2.shape
    n_exp = u.shape[0]
    tt = _tile(t, 512)
    n_sub = 4
    te = n_sub * PEER_KEYS
    s0, e0, s1, e1, thr = sel
    big = pl.BlockSpec((PEER_HEADS, PEER_KEYS, tt), lambda i, e: (0, 0, i))
    row = pl.BlockSpec((tt, d), lambda i, e: (i, 0))
    return pl.pallas_call(
        functools.partial(_peer_expert_kernel, n_sub=n_sub),
        grid=(t // tt, n_exp // te),
        in_specs=[row,
                  pl.BlockSpec((te, d), lambda i, e: (e, 0)),
                  pl.BlockSpec((d, te), lambda i, e: (0, e)),
                  big, big, big, big,
                  pl.BlockSpec((PEER_HEADS, 8, tt), lambda i, e: (0, 0, i)),
                  row,
                  pl.BlockSpec((None, 1, d), lambda i, e: (row_of(i, tt), 0, 0))],
        out_specs=row,
        out_shape=jax.ShapeDtypeStruct((t, d), F32),
        scratch_shapes=[pltpu.VMEM((d, tt), F32), pltpu.VMEM((te, tt), BF16)],
        compiler_params=_params("parallel", "arbitrary"),
        name="peer_experts",
    )(h2, u, vt, s0, e0, s1, e1, thr, x, gate)


def _pad_rope_cols(w):
    half = MLA_ROPE // 2
    z = jnp.zeros(w.shape[:-1] + (half,), w.dtype)
    return jnp.concatenate([w[..., :half], z, w[..., half:], z], axis=-1)


def _layer_weights(i, w_in, mla_cq_g, mla_ckv_g, mla_w_uq, mla_w_ukv, mla_q_g, mla_k_g, gqa_q_g,
                   gqa_k_g, conv_dw_w, conv_dw_b, conv_ln_g, conv_ln_b, w_branch, w_out,
                   peer_w_q, peer_subkeys, peer_u, peer_v):
    w = w_in[i]
    w_kv_part, w_rest = w[:, :KV_COLS], w[:, KV_COLS:]
    w_kv = jnp.concatenate([w_kv_part[:, :OFF_KROPE], w_kv_part[:, OFF_GK:],
                            _pad_rope_cols(w_kv_part[:, OFF_KROPE:OFF_GK])], axis=1)
    uq = mla_w_uq[i].reshape(MLA_Q_LORA, MLA_HEADS, MLA_QK)
    uq = jnp.concatenate([uq[..., :MLA_NOPE], _pad_rope_cols(uq[..., MLA_NOPE:])], axis=-1)
    row = lambda v: v.reshape(1, -1)
    return {
        "w_kv": w_kv.astype(BF16),
        "ckv_g": row(mla_ckv_g[i]),
        "w_ukv": mla_w_ukv[i].astype(BF16),
        "k_g_nope": row(mla_k_g[i, :MLA_NOPE]),
        "k_g_rope": row(_pad_rope_cols(mla_k_g[i, MLA_NOPE:])),
        "gqa_k_g": row(gqa_k_g[i]),
        "w_q": w_rest[:, :OFF_CONV].astype(BF16),
        "cq_g": row(mla_cq_g[i]),
        "w_uq": uq.reshape(MLA_Q_LORA, MLA_HEADS * MLA_HEAD_PAD).astype(BF16),
        "q_g_mla": row(jnp.concatenate([mla_q_g[i, :MLA_NOPE], _pad_rope_cols(mla_q_g[i, MLA_NOPE:])])),
        "gqa_q_g": row(gqa_q_g[i]),
        "w_glu_a": w_rest[:, OFF_CONV:OFF_CONV + CONV_CH].astype(BF16),
        "w_glu_b": w_rest[:, OFF_CONV + CONV_CH:OFF_GATE].astype(BF16),
        "w_gate": w_rest[:, OFF_GATE:].astype(BF16),
        "dw_w": conv_dw_w[i],
        "dw_b": row(conv_dw_b[i]),
        "ln_g": row(conv_ln_g[i]),
        "ln_b": row(conv_ln_b[i]),
        "w_branch": w_branch[i].astype(BF16),
        "w_out": w_out[i].astype(BF16),
        "w_pq": peer_w_q[i].astype(BF16),
        "subkeys": peer_subkeys[i].reshape(PEER_HEADS * 2, PEER_KEYS, PEER_DK // 2).astype(BF16),
        "peer_u": peer_u[i].astype(BF16),
        "peer_vt": peer_v[i].T.astype(BF16),
    }


def _rope_tables(n_lat):
    rows = n_lat // GRID_W
    row = jnp.repeat(jnp.arange(rows, dtype=F32), GRID_W)
    col = jnp.tile(jnp.arange(GRID_W, dtype=F32), rows)

    def angles(rot_dim):
        n_freq = rot_dim // 4
        inv = ROPE_THETA ** (-jnp.arange(n_freq, dtype=F32) / n_freq)
        return jnp.concatenate([row[:, None] * inv, col[:, None] * inv], axis=-1)

    am, ag = angles(MLA_ROPE), angles(GQA_HD)
    ones = jnp.ones_like(am)
    zeros = jnp.zeros_like(am)
    cos_m = jnp.concatenate([jnp.cos(am), ones, jnp.cos(am), ones], axis=-1)
    sin_m = jnp.concatenate([-jnp.sin(am), zeros, jnp.sin(am), zeros], axis=-1)
    cos_g = jnp.concatenate([jnp.cos(ag), jnp.cos(ag)], axis=-1)
    sin_g = jnp.concatenate([-jnp.sin(ag), jnp.sin(ag)], axis=-1)
    return cos_m, sin_m, cos_g, sin_g


def _mixer(x, h, kv_self, kv_ctx, lw, tabs, bsz, n_pos, mod, row_of):
    q_m, q_g = _q_heads(h, lw, tabs, n_pos)
    shape3 = lambda a: a.reshape(bsz, -1, a.shape[-1])
    groups = [kv_ctx, kv_self] if kv_ctx is not None else [kv_self]
    pick = lambda j: [shape3(g[j]) for g in groups]
    o_m = _attention(shape3(q_m), pick(0), pick(1), MLA_HEADS, 1, MLA_HEAD_PAD, MLA_V)
    o_g = _attention(shape3(q_g), pick(2), pick(3), GQA_KV_HEADS, GQA_GROUP, GQA_HD, GQA_HD)
    glu = _glu(h, lw["w_glu_a"], lw["w_glu_b"])
    o_c = _conv(shape3(glu), lw["dw_w"], lw["dw_b"], lw["ln_g"], lw["ln_b"])
    flat = lambda a: a.reshape(-1, a.shape[-1])
    y = _merge(h, flat(o_m), flat(o_c), flat(o_g), lw["w_gate"], lw["w_branch"])
    return _out_proj(y, lw["w_out"], x, mod(2), lw["norm2_g"], mod(3), mod(4), row_of)


def _peer(x, h2, lw, gate, row_of):
    sel = _peer_select(h2, lw["w_pq"], lw["subkeys"])
    return _peer_experts(h2, lw["peer_u"], lw["peer_vt"], sel, x, gate, row_of)


def kernel(x, c, ctx, c_ctx, ada_w, ada_b, norm1_g, norm2_g, w_in, mla_cq_g, mla_ckv_g, mla_w_uq, mla_w_ukv, mla_q_g, mla_k_g, gqa_q_g, gqa_k_g, conv_dw_w, conv_dw_b, conv_ln_g, conv_ln_b, w_branch, w_out, peer_w_q, peer_subkeys, peer_u, peer_v):
    bsz, n_lat, d = x.shape
    n_ctx = ctx.shape[1]
    depth = ada_w.shape[0]
    assert bsz <= CTX_ROW and d == D_MODEL

    cc = jnp.zeros((MOD_ROWS, d), F32).at[:bsz].set(c).at[CTX_ROW].set(c_ctx)
    mods = _modvec(cc, ada_w, ada_b).reshape(depth, MOD_ROWS, N_MOD, 1, d)

    lat_tabs = _rope_tables(n_lat)
    ones = jnp.ones((n_ctx, LANES), F32)
    zeros = jnp.zeros((n_ctx, LANES), F32)
    ctx_tabs = (ones, zeros, ones, zeros)

    lat_row = lambda i, tm: (i * tm) // n_lat
    ctx_row = lambda i, tm: CTX_ROW

    x_lat = x.reshape(bsz * n_lat, d)
    x_ctx = ctx.reshape(bsz * n_ctx, d)
    for i in range(depth):
        lw = _layer_weights(i, w_in, mla_cq_g, mla_ckv_g, mla_w_uq, mla_w_ukv, mla_q_g, mla_k_g,
                            gqa_q_g, gqa_k_g, conv_dw_w, conv_dw_b, conv_ln_g, conv_ln_b, w_branch,
                            w_out, peer_w_q, peer_subkeys, peer_u, peer_v)
        lw["norm2_g"] = norm2_g[i].reshape(1, d)
        mod = lambda idx: mods[i, :, idx]
        g1 = norm1_g[i].reshape(1, d)

        h_ctx = _norm_mod(x_ctx, g1, mod(0), mod(1), ctx_row)
        kv_ctx = _kv_heads(h_ctx, lw, ctx_tabs, n_ctx)
        h_lat = _norm_mod(x_lat, g1, mod(0), mod(1), lat_row)
        kv_lat = _kv_heads(h_lat, lw, lat_tabs, n_lat)

        x_lat, h2 = _mixer(x_lat, h_lat, kv_lat, kv_ctx, lw, lat_tabs, bsz, n_lat, mod, lat_row)
        x_lat = _peer(x_lat, h2, lw, mod(5), lat_row)
        if i < depth - 1:
            x_ctx, h2 = _mixer(x_ctx, h_ctx, kv_ctx, None, lw, ctx_tabs, bsz, n_ctx, mod, ctx_row)
            x_ctx = _peer(x_ctx, h2, lw, mod(5), ctx_row)
    return x_lat.reshape(bsz, n_lat, d)
```

```python
import functools

import jax
import jax.numpy as jnp
from jax import lax
from jax.experimental import pallas as pl
from jax.experimental.pallas import tpu as pltpu

F32 = jnp.float32
BF16 = jnp.bfloat16

D_MODEL = 2048
N_MOD = 6
EPS = 1e-6
GRID_W = 64
ROPE_THETA = 10000.0

MLA_HEADS = 8
MLA_NOPE = 128
MLA_ROPE = 64
MLA_QK = MLA_NOPE + MLA_ROPE
MLA_V = 128
MLA_Q_LORA = 512
MLA_KV_LORA = 256
MLA_HEAD_PAD = 256

CONV_CH = 1024
CONV_WIDTH = 31
CONV_HALO = 16

GQA_HEADS = 8
GQA_KV_HEADS = 2
GQA_GROUP = GQA_HEADS // GQA_KV_HEADS
GQA_HD = 128

N_BRANCH = 3
BRANCH_W = 1024

OFF_KROPE = MLA_KV_LORA
OFF_GK = OFF_KROPE + MLA_ROPE
OFF_GV = OFF_GK + GQA_KV_HEADS * GQA_HD
KV_COLS = OFF_GV + GQA_KV_HEADS * GQA_HD
OFF_GQ = MLA_Q_LORA
OFF_CONV = OFF_GQ + GQA_HEADS * GQA_HD
OFF_GATE = OFF_CONV + 2 * CONV_CH

PEER_HEADS = 8
PEER_KEYS = 128
PEER_EXPERTS = PEER_KEYS * PEER_KEYS
PEER_TOPK = 16
PEER_DK = 256

LANES = 128
MOD_ROWS = 16
CTX_ROW = 4
VMEM_LIMIT = 56 * 1024 * 1024
NEG_INF = float("-inf")
LOG2_E = 1.4426950408889634

_NT = (((1,), (1,)), ((), ()))


def _params(*sem, flags=None):
    return pltpu.CompilerParams(dimension_semantics=sem, vmem_limit_bytes=VMEM_LIMIT, flags=flags)


def _tile(n, want):
    t = min(n, want)
    assert n % t == 0, (n, t)
    return t


def _dot(a, b):
    return jnp.dot(a, b, preferred_element_type=F32)


def _rms(x, n):
    return x * lax.rsqrt(jnp.sum(x * x, axis=-1, keepdims=True) * (1.0 / n) + EPS)


def _rope(x, cos, sin_signed):
    return x * cos + pltpu.roll(x, LANES // 2, 1) * sin_signed


def _modvec_kernel(c_ref, w_ref, b_ref, o_ref):
    c = c_ref[...]
    a = c * jax.nn.sigmoid(c)
    a_hi = a.astype(BF16)
    a_lo = (a - a_hi.astype(F32)).astype(BF16)
    w = w_ref[...]
    w_hi = w.astype(BF16)
    w_lo = (w - w_hi.astype(F32)).astype(BF16)
    o_ref[...] = _dot(a_hi, w_hi) + _dot(a_lo, w_hi) + _dot(a_hi, w_lo) + b_ref[...]


def _modvec(cc, ada_w, ada_b):
    depth, d, n = ada_w.shape
    tn = _tile(n, 1024)
    return pl.pallas_call(
        _modvec_kernel,
        grid=(depth, n // tn),
        in_specs=[
            pl.BlockSpec((MOD_ROWS, d), lambda l, j: (0, 0)),
            pl.BlockSpec((None, d, tn), lambda l, j: (l, 0, j)),
            pl.BlockSpec((None, 1, tn), lambda l, j: (l, 0, j)),
        ],
        out_specs=pl.BlockSpec((None, MOD_ROWS, tn), lambda l, j: (l, 0, j)),
        out_shape=jax.ShapeDtypeStruct((depth, MOD_ROWS, n), F32),
        compiler_params=_params("parallel", "parallel"),
        name="modvec",
    )(cc, ada_w, ada_b.reshape(depth, 1, n))


def _norm_mod_kernel(x_ref, g_ref, sh_ref, sc_ref, o_ref):
    y = _rms(x_ref[...], D_MODEL) * g_ref[...]
    o_ref[...] = (y * (1.0 + sc_ref[...]) + sh_ref[...]).astype(BF16)


def _norm_mod(x, g, shift, scale, row_of):
    t, d = x.shape
    tm = _tile(t, 512)
    mod_spec = pl.BlockSpec((None, 1, d), lambda i: (row_of(i, tm), 0, 0))
    return pl.pallas_call(
        _norm_mod_kernel,
        grid=(t // tm,),
        in_specs=[
            pl.BlockSpec((tm, d), lambda i: (i, 0)),
            pl.BlockSpec((1, d), lambda i: (0, 0)),
            mod_spec,
            mod_spec,
        ],
        out_specs=pl.BlockSpec((tm, d), lambda i: (i, 0)),
        out_shape=jax.ShapeDtypeStruct((t, d), BF16),
        compiler_params=_params("parallel"),
        name="norm_mod",
    )(x, g, shift, scale)


def _kv_kernel(h_ref, wkv_ref, ckvg_ref, wukv_ref, kgn_ref, kgr_ref, gkg_ref,
               cm_ref, sm_ref, cg_ref, sg_ref, km_ref, vm_ref, kg_ref, vg_ref):
    z = _dot(h_ref[...], wkv_ref[...])
    ckv = _rms(z[:, :MLA_KV_LORA], MLA_KV_LORA) * ckvg_ref[...]
    kv = _dot(ckv.astype(BF16), wukv_ref[...])
    kr = z[:, 768:896]
    kr_ss = jnp.sum(kr * kr, axis=-1, keepdims=True)
    kr_rot = _rope(kr * kgr_ref[...], cm_ref[...], sm_ref[...])
    kgn = kgn_ref[...]
    ones = jnp.ones((h_ref.shape[0], LANES), BF16)
    for h in range(MLA_HEADS):
        c0 = h * MLA_HEAD_PAD
        kn = kv[:, c0:c0 + MLA_NOPE]
        r = lax.rsqrt((jnp.sum(kn * kn, axis=-1, keepdims=True) + kr_ss) * (1.0 / MLA_QK) + EPS)
        km_ref[:, c0:c0 + MLA_NOPE] = (kn * r * kgn).astype(BF16)
        km_ref[:, c0 + MLA_NOPE:c0 + MLA_HEAD_PAD] = (kr_rot * r).astype(BF16)
        vm_ref[:, 2 * h * MLA_V:(2 * h + 1) * MLA_V] = kv[:, c0 + MLA_NOPE:c0 + MLA_HEAD_PAD].astype(BF16)
        vm_ref[:, (2 * h + 1) * MLA_V:(2 * h + 2) * MLA_V] = ones
    for j in range(GQA_KV_HEADS):
        gk = _rms(z[:, 256 + j * GQA_HD:256 + (j + 1) * GQA_HD], GQA_HD) * gkg_ref[...]
        kg_ref[:, j * GQA_HD:(j + 1) * GQA_HD] = _rope(gk, cg_ref[...], sg_ref[...]).astype(BF16)
        vg_ref[:, 2 * j * GQA_HD:(2 * j + 1) * GQA_HD] = z[:, 512 + j * GQA_HD:512 + (j + 1) * GQA_HD].astype(BF16)
        vg_ref[:, (2 * j + 1) * GQA_HD:(2 * j + 2) * GQA_HD] = ones


def _kv_heads(h, lw, tabs, n_pos):
    t, d = h.shape
    tm = _tile(n_pos, 512)
    npt = n_pos // tm
    full = lambda a: pl.BlockSpec(a.shape, lambda i: (0,) * a.ndim)
    tab = pl.BlockSpec((tm, LANES), lambda i: (i % npt, 0))
    row = lambda w: pl.BlockSpec((tm, w), lambda i: (i, 0))
    ws = [lw["w_kv"], lw["ckv_g"], lw["w_ukv"], lw["k_g_nope"], lw["k_g_rope"], lw["gqa_k_g"]]
    return pl.pallas_call(
        _kv_kernel,
        grid=(t // tm,),
        in_specs=[row(d)] + [full(w) for w in ws] + [tab] * 4,
        out_specs=[row(MLA_HEADS * MLA_HEAD_PAD), row(MLA_HEADS * 2 * MLA_V),
                   row(GQA_KV_HEADS * GQA_HD), row(GQA_KV_HEADS * 2 * GQA_HD)],
        out_shape=[jax.ShapeDtypeStruct((t, MLA_HEADS * MLA_HEAD_PAD), BF16),
                   jax.ShapeDtypeStruct((t, MLA_HEADS * 2 * MLA_V), BF16),
                   jax.ShapeDtypeStruct((t, GQA_KV_HEADS * GQA_HD), BF16),
                   jax.ShapeDtypeStruct((t, GQA_KV_HEADS * 2 * GQA_HD), BF16)],
        compiler_params=_params("parallel"),
        name="kv_heads",
    )(h, *ws, *tabs)


def _q_kernel(h_ref, wq_ref, cqg_ref, wuq_ref, qgm_ref, qgg_ref,
              cm_ref, sm_ref, cg_ref, sg_ref, qm_ref, qg_ref):
    z = _dot(h_ref[...], wq_ref[...])
    cq = _rms(z[:, :MLA_Q_LORA], MLA_Q_LORA) * cqg_ref[...]
    qq = _dot(cq.astype(BF16), wuq_ref[...])
    qgm = qgm_ref[...]
    for h in range(MLA_HEADS):
        c0 = h * MLA_HEAD_PAD
        seg = _rms(qq[:, c0:c0 + MLA_HEAD_PAD], MLA_QK) * qgm * (LOG2_E * MLA_QK ** -0.5)
        qm_ref[:, c0:c0 + MLA_NOPE] = seg[:, :MLA_NOPE].astype(BF16)
        qm_ref[:, c0 + MLA_NOPE:c0 + MLA_HEAD_PAD] = _rope(
            seg[:, MLA_NOPE:], cm_ref[...], sm_ref[...]).astype(BF16)
    for h in range(GQA_HEADS):
        c0 = MLA_Q_LORA + h * GQA_HD
        seg = _rms(z[:, c0:c0 + GQA_HD], GQA_HD) * qgg_ref[...] * (LOG2_E * GQA_HD ** -0.5)
        qg_ref[:, h * GQA_HD:(h + 1) * GQA_HD] = _rope(seg, cg_ref[...], sg_ref[...]).astype(BF16)


def _q_heads(h, lw, tabs, n_pos):
    t, d = h.shape
    tm = _tile(n_pos, 512)
    npt = n_pos // tm
    full = lambda a: pl.BlockSpec(a.shape, lambda i: (0,) * a.ndim)
    tab = pl.BlockSpec((tm, LANES), lambda i: (i % npt, 0))
    row = lambda w: pl.BlockSpec((tm, w), lambda i: (i, 0))
    ws = [lw["w_q"], lw["cq_g"], lw["w_uq"], lw["q_g_mla"], lw["gqa_q_g"]]
    return pl.pallas_call(
        _q_kernel,
        grid=(t // tm,),
        in_specs=[row(d)] + [full(w) for w in ws] + [tab] * 4,
        out_specs=[row(MLA_HEADS * MLA_HEAD_PAD), row(GQA_HEADS * GQA_HD)],
        out_shape=[jax.ShapeDtypeStruct((t, MLA_HEADS * MLA_HEAD_PAD), BF16),
                   jax.ShapeDtypeStruct((t, GQA_HEADS * GQA_HD), BF16)],
        compiler_params=_params("parallel"),
        name="q_heads",
    )(h, *ws, *tabs)


ATTN_SUB = 256


def _attn_kernel(*refs, n_kv, group, dq, dv):
    q_ref, o_ref = refs[0], refs[-1]
    k_refs = refs[1:1 + n_kv]
    v_refs = refs[1 + n_kv:1 + 2 * n_kv]
    tq = q_ref.shape[0]
    sub = min(tq, ATTN_SUB)
    for g in range(group):
        for r in range(tq // sub):
            rows = slice(r * sub, (r + 1) * sub)
            q = q_ref[rows, g * dq:(g + 1) * dq]
            s = [lax.dot_general(q, k[...], _NT, preferred_element_type=F32) for k in k_refs]
            m = functools.reduce(jnp.maximum, [jnp.max(x, axis=-1, keepdims=True) for x in s])
            o = functools.reduce(jnp.add, [_dot(jnp.exp2(x - m).astype(BF16), v[...])
                                           for x, v in zip(s, v_refs)])
            o_ref[rows, g * dv:(g + 1) * dv] = (o[:, :dv] / o[:, dv:]).astype(BF16)


def _attention(q, ks, vs, n_kv_heads, group, dq, dv):
    b, l, _ = q.shape
    tq = _tile(l, 512)
    kspec = lambda a, w: pl.BlockSpec((None, a.shape[1], w), lambda bi, hi, i: (bi, 0, hi))
    return pl.pallas_call(
        functools.partial(_attn_kernel, n_kv=len(ks), group=group, dq=dq, dv=dv),
        grid=(b, n_kv_heads, l // tq),
        in_specs=[pl.BlockSpec((None, tq, group * dq), lambda bi, hi, i: (bi, i, hi))]
        + [kspec(k, dq) for k in ks] + [kspec(v, 2 * dv) for v in vs],
        out_specs=pl.BlockSpec((None, tq, group * dv), lambda bi, hi, i: (bi, i, hi)),
        out_shape=jax.ShapeDtypeStruct((b, l, n_kv_heads * group * dv), BF16),
        compiler_params=_params("parallel", "parallel", "parallel"),
        name="attention",
    )(q, *ks, *vs)


def _glu_kernel(h_ref, wa_ref, wb_ref, o_ref):
    h = h_ref[...]
    o_ref[...] = (_dot(h, wa_ref[...]) * jax.nn.sigmoid(_dot(h, wb_ref[...]))).astype(BF16)


def _glu(h, wa, wb):
    t, d = h.shape
    tm = _tile(t, 512)
    full = lambda a: pl.BlockSpec(a.shape, lambda i: (0, 0))
    return pl.pallas_call(
        _glu_kernel,
        grid=(t // tm,),
        in_specs=[pl.BlockSpec((tm, d), lambda i: (i, 0)), full(wa), full(wb)],
        out_specs=pl.BlockSpec((tm, CONV_CH), lambda i: (i, 0)),
        out_shape=jax.ShapeDtypeStruct((t, CONV_CH), BF16),
        compiler_params=_params("parallel"),
        name="glu",
    )(h, wa, wb)


def _conv_kernel(x_ref, w_ref, b_ref, g_ref, beta_ref, o_ref, pad_ref, *, tl):
    i = pl.program_id(1)
    last = pl.num_programs(1) - 1
    start = pl.multiple_of(i * tl, tl)
    pad_ref[CONV_HALO:CONV_HALO + tl, :] = x_ref[pl.ds(start, tl), :].astype(F32)
    zeros = jnp.zeros((CONV_HALO, CONV_CH), F32)

    @pl.when(i == 0)
    def _():
        pad_ref[0:CONV_HALO, :] = zeros

    @pl.when(i > 0)
    def _():
        pad_ref[0:CONV_HALO, :] = x_ref[pl.ds(start - CONV_HALO, CONV_HALO), :].astype(F32)

    @pl.when(i == last)
    def _():
        pad_ref[CONV_HALO + tl:, :] = zeros

    @pl.when(i < last)
    def _():
        pad_ref[CONV_HALO + tl:, :] = x_ref[pl.ds(start + tl, CONV_HALO), :].astype(F32)

    off = CONV_HALO - CONV_WIDTH // 2
    acc = jnp.zeros((tl, CONV_CH), F32) + b_ref[...]
    for k in range(CONV_WIDTH):
        acc = acc + pad_ref[off + k:off + k + tl, :] * w_ref[k:k + 1, :]
    mu = jnp.mean(acc, axis=-1, keepdims=True)
    xc = acc - mu
    y = xc * lax.rsqrt(jnp.mean(xc * xc, axis=-1, keepdims=True) + EPS)
    y = y * g_ref[...] + beta_ref[...]
    o_ref[...] = (y * jax.nn.sigmoid(y)).astype(BF16)


def _conv(glu, w, b, g, beta):
    bsz, l, ch = glu.shape
    tl = _tile(l, 256)
    vec = pl.BlockSpec((1, ch), lambda bi, i: (0, 0))
    return pl.pallas_call(
        functools.partial(_conv_kernel, tl=tl),
        grid=(bsz, l // tl),
        in_specs=[pl.BlockSpec((None, l, ch), lambda bi, i: (bi, 0, 0)),
                  pl.BlockSpec((CONV_WIDTH, ch), lambda bi, i: (0, 0)), vec, vec, vec],
        out_specs=pl.BlockSpec((None, tl, ch), lambda bi, i: (bi, i, 0)),
        out_shape=jax.ShapeDtypeStruct((bsz, l, ch), BF16),
        scratch_shapes=[pltpu.VMEM((tl + 2 * CONV_HALO, ch), F32)],
        compiler_params=_params("parallel", "arbitrary"),
        name="conv",
    )(glu, w, b, g, beta)


def _merge_kernel(h_ref, om_ref, oc_ref, og_ref, g0_ref, g1_ref, g2_ref, wb_ref, y_ref):
    h = h_ref[...]
    y = jax.nn.sigmoid(_dot(h, g0_ref[...])) * _dot(om_ref[...], wb_ref[0])
    y = y + jax.nn.sigmoid(_dot(h, g1_ref[...])) * _dot(oc_ref[...], wb_ref[1])
    y = y + jax.nn.sigmoid(_dot(h, g2_ref[...])) * _dot(og_ref[...], wb_ref[2])
    y_ref[...] = y.astype(BF16)


def _merge(h, o_m, o_c, o_g, w_gate, w_branch):
    t, d = h.shape
    tm = _tile(t, 512)
    tn = 512
    nb = d // tn
    row = lambda w: pl.BlockSpec((tm, w), lambda i, j: (i, 0))
    gate = lambda br: pl.BlockSpec((d, tn), lambda i, j: (0, br * nb + j))
    return pl.pallas_call(
        _merge_kernel,
        grid=(t // tm, nb),
        in_specs=[row(d), row(BRANCH_W), row(BRANCH_W), row(BRANCH_W), gate(0), gate(1), gate(2),
                  pl.BlockSpec((N_BRANCH, BRANCH_W, tn), lambda i, j: (0, 0, j))],
        out_specs=pl.BlockSpec((tm, tn), lambda i, j: (i, j)),
        out_shape=jax.ShapeDtypeStruct((t, d), BF16),
        compiler_params=_params("parallel", "arbitrary"),
        name="merge",
    )(h, o_m, o_c, o_g, w_gate, w_gate, w_gate, w_branch)


def _out_kernel(y_ref, w_ref, x_ref, gate_ref, g_ref, sh_ref, sc_ref, xo_ref, ho_ref):
    x = x_ref[...] + gate_ref[...] * _dot(y_ref[...], w_ref[...])
    xo_ref[...] = x
    hn = _rms(x, D_MODEL) * g_ref[...]
    ho_ref[...] = (hn * (1.0 + sc_ref[...]) + sh_ref[...]).astype(BF16)


def _out_proj(y, w_out, x, gate, g2, shift, scale, row_of):
    t, d = x.shape
    tm = _tile(t, 256)
    row = pl.BlockSpec((tm, d), lambda i: (i, 0))
    mod = pl.BlockSpec((None, 1, d), lambda i: (row_of(i, tm), 0, 0))
    return pl.pallas_call(
        _out_kernel,
        grid=(t // tm,),
        in_specs=[row, pl.BlockSpec((d, d), lambda i: (0, 0)), row, mod,
                  pl.BlockSpec((1, d), lambda i: (0, 0)), mod, mod],
        out_specs=[row, row],
        out_shape=[jax.ShapeDtypeStruct((t, d), F32), jax.ShapeDtypeStruct((t, d), BF16)],
        compiler_params=_params("parallel"),
        name="out_proj",
    )(y, w_out, x, gate, g2, shift, scale)


def _batcher_pairs(n):
    pairs = []
    p = 1
    while p < n:
        k = p
        while k >= 1:
            for j in range(k % p, n - k, 2 * k):
                for i in range(min(k, n - j - k)):
                    if (i + j) // (2 * p) == (i + j + k) // (2 * p):
                        pairs.append((i + j, i + j + k))
            k //= 2
        p *= 2
    return pairs


_SORT16 = _batcher_pairs(PEER_TOPK)
_BITONIC16 = [(i, i + st) for st in (8, 4, 2, 1) for i in range(PEER_TOPK) if not i & st]
SUBLANES = 8


def _exchange(xs, i, j):
    a, b = xs[i], xs[j]
    if a is None:
        xs[i], xs[j] = b, None
    elif b is not None:
        xs[i], xs[j] = jnp.maximum(a, b), jnp.minimum(a, b)


def _top16_of_slabs(xs):
    xs = list(xs)
    for i, j in _SORT16:
        _exchange(xs, i, j)
    xs = [jnp.full_like(xs[0], NEG_INF) if x is None else x for x in xs]
    for shift in (4, 2, 1):
        other = [pltpu.roll(x, shift, 0) for x in xs]
        xs = [jnp.maximum(xs[v], other[PEER_TOPK - 1 - v]) for v in range(PEER_TOPK)]
        for i, j in _BITONIC16:
            _exchange(xs, i, j)
    return xs


def _stack_sublanes(slabs, sub_iota):
    out = slabs[SUBLANES - 1]
    for r in range(SUBLANES - 1):
        out = jnp.where(sub_iota == r, slabs[r], out)
    return out


def _peer_select_kernel(h_ref, wq_ref, sub_ref, s0_ref, e0_ref, s1_ref, e1_ref, thr_ref):
    tm = h_ref.shape[0]
    q = _dot(h_ref[...], wq_ref[...]).astype(BF16)
    sub_iota = lax.broadcasted_iota(jnp.int32, (SUBLANES, tm), 0)
    slabs = lambda x: [x[SUBLANES * v:SUBLANES * (v + 1)] for v in range(PEER_KEYS // SUBLANES)]
    for h in range(PEER_HEADS):
        s, top = [], []
        for p in range(2):
            hp = 2 * h + p
            sc = lax.dot_general(sub_ref[hp], q[:, hp * LANES:(hp + 1) * LANES], _NT,
                                 preferred_element_type=F32)
            s.append(slabs(sc))
            top.append(_top16_of_slabs(s[p]))
        v0, v1 = top
        v1_lo = _stack_sublanes(v1[:SUBLANES], sub_iota)
        v1_hi = _stack_sublanes(v1[SUBLANES:], sub_iota)
        v0_hi = _stack_sublanes(v0[SUBLANES:], sub_iota)
        cand = ([v0[0] + v1_lo, v0[0] + v1_hi] + [v0[a] + v1_lo for a in range(1, SUBLANES)]
                + [v0_hi + v1[0]])
        best = _top16_of_slabs(cand + [None] * (PEER_TOPK - len(cand)))
        z = functools.reduce(jnp.add, [jnp.exp(c - best[0]) for c in best])
        e1_scale = 0.5 / z
        for v in range(PEER_KEYS // SUBLANES):
            rows = slice(SUBLANES * v, SUBLANES * (v + 1))
            s0_ref[h, rows, :] = s[0][v]
            s1_ref[h, rows, :] = s[1][v]
            e0_ref[h, rows, :] = jnp.exp(s[0][v] - v0[0])
            e1_ref[h, rows, :] = jnp.exp(s[1][v] - v1[0]) * e1_scale
        thr_ref[h] = best[PEER_TOPK - 1]


def _peer_select(h2, w_pq, subkeys):
    t, d = h2.shape
    tm = _tile(t, 256)
    big = pl.BlockSpec((PEER_HEADS, PEER_KEYS, tm), lambda i: (0, 0, i))
    big_shape = jax.ShapeDtypeStruct((PEER_HEADS, PEER_KEYS, t), F32)
    return pl.pallas_call(
        _peer_select_kernel,
        grid=(t // tm,),
        in_specs=[pl.BlockSpec((tm, d), lambda i: (i, 0)),
                  pl.BlockSpec(w_pq.shape, lambda i: (0, 0)),
                  pl.BlockSpec(subkeys.shape, lambda i: (0, 0, 0))],
        out_specs=[big, big, big, big, pl.BlockSpec((PEER_HEADS, 8, tm), lambda i: (0, 0, i))],
        out_shape=[big_shape, big_shape, big_shape, big_shape,
                   jax.ShapeDtypeStruct((PEER_HEADS, 8, t), F32)],
        compiler_params=_params("parallel"),
        name="peer_select",
    )(h2, w_pq, subkeys)


GELU_C = 0.7978845608028654


def _peer_expert_kernel(h_ref, u_ref, vt_ref, s0_ref, e0_ref, s1_ref, e1_ref, thr_ref,
                        x_ref, gate_ref, o_ref, acc_ref, coef_ref, *, n_sub):
    e = pl.program_id(1)
    tt = h_ref.shape[0]

    @pl.when(e == 0)
    def _():
        acc_ref[...] = jnp.zeros_like(acc_ref)

    a_t = lax.dot_general(u_ref[...], h_ref[...], _NT, preferred_element_type=F32)
    for ii in range(n_sub):
        i = e * n_sub + ii
        w = jnp.zeros((PEER_KEYS, tt), F32)
        for h in range(PEER_HEADS):
            total = s1_ref[h] + s0_ref[h, pl.ds(i, 1), :]
            val = e1_ref[h] * e0_ref[h, pl.ds(i, 1), :]
            w = w + jnp.where(total >= thr_ref[h, 0:1, :], val, 0.0)
        a = a_t[ii * PEER_KEYS:(ii + 1) * PEER_KEYS]
        g = a * (1.0 + jnp.tanh(a * (GELU_C + (GELU_C * 0.044715) * (a * a))))
        coef_ref[ii * PEER_KEYS:(ii + 1) * PEER_KEYS, :] = (w * g).astype(BF16)
    acc_ref[...] += _dot(vt_ref[...], coef_ref[...])

    @pl.when(e == pl.num_programs(1) - 1)
    def _():
        o_ref[...] = x_ref[...] + gate_ref[...] * acc_ref[...].T


def _peer_experts(h2, u, vt, sel, x, gate, row_of):
    t, d = h2.shape
    n_exp = u.shape[0]
    tt = _tile(t, 512)
    n_sub = 8
    te = n_sub * PEER_KEYS
    s0, e0, s1, e1, thr = sel
    once = pl.Buffered(1)
    big = pl.BlockSpec((PEER_HEADS, PEER_KEYS, tt), lambda i, e: (0, 0, i), pipeline_mode=once)
    return pl.pallas_call(
        functools.partial(_peer_expert_kernel, n_sub=n_sub),
        grid=(t // tt, n_exp // te),
        in_specs=[pl.BlockSpec((tt, d), lambda i, e: (i, 0), pipeline_mode=once),
                  pl.BlockSpec((te, d), lambda i, e: (e, 0)),
                  pl.BlockSpec((d, te), lambda i, e: (0, e)),
                  big, big, big, big,
                  pl.BlockSpec((PEER_HEADS, 8, tt), lambda i, e: (0, 0, i), pipeline_mode=once),
                  pl.BlockSpec((tt, d), lambda i, e: (i, 0), pipeline_mode=once),
                  pl.BlockSpec((None, 1, d), lambda i, e: (row_of(i, tt), 0, 0))],
        out_specs=pl.BlockSpec((tt, d), lambda i, e: (i, 0)),
        out_shape=jax.ShapeDtypeStruct((t, d), F32),
        scratch_shapes=[pltpu.VMEM((d, tt), F32), pltpu.VMEM((te, tt), BF16)],
        compiler_params=_params("parallel", "arbitrary"),
        name="peer_experts",
    )(h2, u, vt, s0, e0, s1, e1, thr, x, gate)


def _pad_rope_cols(w):
    half = MLA_ROPE // 2
    z = jnp.zeros(w.shape[:-1] + (half,), w.dtype)
    return jnp.concatenate([w[..., :half], z, w[..., half:], z], axis=-1)


def _layer_weights(i, w_in, mla_cq_g, mla_ckv_g, mla_w_uq, mla_w_ukv, mla_q_g, mla_k_g, gqa_q_g,
                   gqa_k_g, conv_dw_w, conv_dw_b, conv_ln_g, conv_ln_b, w_branch, w_out,
                   peer_w_q, peer_subkeys, peer_u, peer_v):
    w = w_in[i]
    w_kv_part, w_rest = w[:, :KV_COLS], w[:, KV_COLS:]
    w_kv = jnp.concatenate([w_kv_part[:, :OFF_KROPE], w_kv_part[:, OFF_GK:],
                            _pad_rope_cols(w_kv_part[:, OFF_KROPE:OFF_GK])], axis=1)
    uq = mla_w_uq[i].reshape(MLA_Q_LORA, MLA_HEADS, MLA_QK)
    uq = jnp.concatenate([uq[..., :MLA_NOPE], _pad_rope_cols(uq[..., MLA_NOPE:])], axis=-1)
    row = lambda v: v.reshape(1, -1)
    return {
        "w_kv": w_kv.astype(BF16),
        "ckv_g": row(mla_ckv_g[i]),
        "w_ukv": mla_w_ukv[i].astype(BF16),
        "k_g_nope": row(mla_k_g[i, :MLA_NOPE]),
        "k_g_rope": row(_pad_rope_cols(mla_k_g[i, MLA_NOPE:])),
        "gqa_k_g": row(gqa_k_g[i]),
        "w_q": w_rest[:, :OFF_CONV].astype(BF16),
        "cq_g": row(mla_cq_g[i]),
        "w_uq": uq.reshape(MLA_Q_LORA, MLA_HEADS * MLA_HEAD_PAD).astype(BF16),
        "q_g_mla": row(jnp.concatenate([mla_q_g[i, :MLA_NOPE], _pad_rope_cols(mla_q_g[i, MLA_NOPE:])])),
        "gqa_q_g": row(gqa_q_g[i]),
        "w_glu_a": w_rest[:, OFF_CONV:OFF_CONV + CONV_CH].astype(BF16),
        "w_glu_b": w_rest[:, OFF_CONV + CONV_CH:OFF_GATE].astype(BF16),
        "w_gate": w_rest[:, OFF_GATE:].astype(BF16),
        "dw_w": conv_dw_w[i],
        "dw_b": row(conv_dw_b[i]),
        "ln_g": row(conv_ln_g[i]),
        "ln_b": row(conv_ln_b[i]),
        "w_branch": w_branch[i].astype(BF16),
        "w_out": w_out[i].astype(BF16),
        "w_pq": peer_w_q[i].astype(BF16),
        "subkeys": peer_subkeys[i].reshape(PEER_HEADS * 2, PEER_KEYS, PEER_DK // 2).astype(BF16),
        "peer_u": peer_u[i].astype(BF16),
        "peer_vt": peer_v[i].T.astype(BF16),
    }


def _rope_tables(n_lat):
    rows = n_lat // GRID_W
    row = jnp.repeat(jnp.arange(rows, dtype=F32), GRID_W)
    col = jnp.tile(jnp.arange(GRID_W, dtype=F32), rows)

    def angles(rot_dim):
        n_freq = rot_dim // 4
        inv = ROPE_THETA ** (-jnp.arange(n_freq, dtype=F32) / n_freq)
        return jnp.concatenate([row[:, None] * inv, col[:, None] * inv], axis=-1)

    am, ag = angles(MLA_ROPE), angles(GQA_HD)
    ones = jnp.ones_like(am)
    zeros = jnp.zeros_like(am)
    cos_m = jnp.concatenate([jnp.cos(am), ones, jnp.cos(am), ones], axis=-1)
    sin_m = jnp.concatenate([-jnp.sin(am), zeros, jnp.sin(am), zeros], axis=-1)
    cos_g = jnp.concatenate([jnp.cos(ag), jnp.cos(ag)], axis=-1)
    sin_g = jnp.concatenate([-jnp.sin(ag), jnp.sin(ag)], axis=-1)
    return cos_m, sin_m, cos_g, sin_g


def _mixer(x, h, kv_self, kv_ctx, lw, tabs, bsz, n_pos, mod, row_of):
    q_m, q_g = _q_heads(h, lw, tabs, n_pos)
    shape3 = lambda a: a.reshape(bsz, -1, a.shape[-1])
    groups = [kv_ctx, kv_self] if kv_ctx is not None else [kv_self]
    pick = lambda j: [shape3(g[j]) for g in groups]
    o_m = _attention(shape3(q_m), pick(0), pick(1), MLA_HEADS, 1, MLA_HEAD_PAD, MLA_V)
    o_g = _attention(shape3(q_g), pick(2), pick(3), GQA_KV_HEADS, GQA_GROUP, GQA_HD, GQA_HD)
    glu = _glu(h, lw["w_glu_a"], lw["w_glu_b"])
    o_c = _conv(shape3(glu), lw["dw_w"], lw["dw_b"], lw["ln_g"], lw["ln_b"])
    flat = lambda a: a.reshape(-1, a.shape[-1])
    y = _merge(h, flat(o_m), flat(o_c), flat(o_g), lw["w_gate"], lw["w_branch"])
    return _out_proj(y, lw["w_out"], x, mod(2), lw["norm2_g"], mod(3), mod(4), row_of)


def _peer(x, h2, lw, gate, row_of):
    sel = _peer_select(h2, lw["w_pq"], lw["subkeys"])
    return _peer_experts(h2, lw["peer_u"], lw["peer_vt"], sel, x, gate, row_of)


def kernel(x, c, ctx, c_ctx, ada_w, ada_b, norm1_g, norm2_g, w_in, mla_cq_g, mla_ckv_g, mla_w_uq, mla_w_ukv, mla_q_g, mla_k_g, gqa_q_g, gqa_k_g, conv_dw_w, conv_dw_b, conv_ln_g, conv_ln_b, w_branch, w_out, peer_w_q, peer_subkeys, peer_u, peer_v):
    bsz, n_lat, d = x.shape
    n_ctx = ctx.shape[1]
    depth = ada_w.shape[0]
    assert bsz <= CTX_ROW and d == D_MODEL

    cc = jnp.zeros((MOD_ROWS, d), F32).at[:bsz].set(c).at[CTX_ROW].set(c_ctx)
    mods = _modvec(cc, ada_w, ada_b).reshape(depth, MOD_ROWS, N_MOD, 1, d)

    lat_tabs = _rope_tables(n_lat)
    ones = jnp.ones((n_ctx, LANES), F32)
    zeros = jnp.zeros((n_ctx, LANES), F32)
    ctx_tabs = (ones, zeros, ones, zeros)

    lat_row = lambda i, tm: (i * tm) // n_lat
    ctx_row = lambda i, tm: CTX_ROW

    x_lat = x.reshape(bsz * n_lat, d)
    x_ctx = ctx.reshape(bsz * n_ctx, d)
    for i in range(depth):
        lw = _layer_weights(i, w_in, mla_cq_g, mla_ckv_g, mla_w_uq, mla_w_ukv, mla_q_g, mla_k_g,
                            gqa_q_g, gqa_k_g, conv_dw_w, conv_dw_b, conv_ln_g, conv_ln_b, w_branch,
                            w_out, peer_w_q, peer_subkeys, peer_u, peer_v)
        lw["norm2_g"] = norm2_g[i].reshape(1, d)
        mod = lambda idx: mods[i, :, idx]
        g1 = norm1_g[i].reshape(1, d)

        h_ctx = _norm_mod(x_ctx, g1, mod(0), mod(1), ctx_row)
        kv_ctx = _kv_heads(h_ctx, lw, ctx_tabs, n_ctx)
        h_lat = _norm_mod(x_lat, g1, mod(0), mod(1), lat_row)
        kv_lat = _kv_heads(h_lat, lw, lat_tabs, n_lat)

        x_lat, h2 = _mixer(x_lat, h_lat, kv_lat, kv_ctx, lw, lat_tabs, bsz, n_lat, mod, lat_row)
        x_lat = _peer(x_lat, h2, lw, mod(5), lat_row)
        if i < depth - 1:
            x_ctx, h2 = _mixer(x_ctx, h_ctx, kv_ctx, None, lw, ctx_tabs, bsz, n_ctx, mod, ctx_row)
            x_ctx = _peer(x_ctx, h2, lw, mod(5), ctx_row)
    return x_lat.reshape(bsz, n_lat, d)
```
